```python
import math
import jax, jax.numpy as jnp
from jax import lax
import numpy as np

D_MODEL = 1024
BATCH = 4
SEQ = 4096
DEPTH = 2

CHUNK = 64
LEFT_CHUNKS = 8
BAND = LEFT_CHUNKS + 1
ATTN_HEAD_DIM = 64
ATTN_HEADS = D_MODEL // ATTN_HEAD_DIM
MAX_REL = 256
DN_HEAD_DIM = 128
DN_HEADS = D_MODEL // DN_HEAD_DIM
DN_WIDTH = DN_HEADS * DN_HEAD_DIM
CONV_K = 4
D_FF = 4 * D_MODEL
EPS = 1e-6
N_MIXERS = 2
N_ATTN_LAYERS = (DEPTH + 1) // 2
N_DN_LAYERS = DEPTH // 2

kernel_name = "hybrid_chunk_attn_gated_deltanet_encoder"


def rmsnorm(x, w):
    xf = x.astype(jnp.float32)
    y = xf * lax.rsqrt(jnp.mean(xf * xf, axis=-1, keepdims=True) + EPS)
    return (y * w.astype(jnp.float32)).astype(x.dtype)


def l2norm(x):
    xf = x.astype(jnp.float32)
    return xf * lax.rsqrt(jnp.sum(xf * xf, axis=-1, keepdims=True) + EPS)


def chunked_rel_attention(h, w_qkv, rel_bias, w_o):
    B, S, _ = h.shape
    nc = S // CHUNK
    qkv = h @ w_qkv
    q, k, v = jnp.split(qkv, 3, axis=-1)
    q = q.reshape(B, S, ATTN_HEADS, ATTN_HEAD_DIM)
    k = k.reshape(B, S, ATTN_HEADS, ATTN_HEAD_DIM)
    v = v.reshape(B, S, ATTN_HEADS, ATTN_HEAD_DIM)
    pad = ((0, 0), (LEFT_CHUNKS * CHUNK, 0), (0, 0), (0, 0))
    kpad = jnp.pad(k, pad)
    vpad = jnp.pad(v, pad)
    q_off = jnp.arange(CHUNK)[:, None]
    k_off = jnp.arange(BAND * CHUNK)[None, :] - LEFT_CHUNKS * CHUNK
    dist = jnp.clip(q_off - k_off, -MAX_REL, MAX_REL) + MAX_REL
    bias = rel_bias[:, dist].astype(jnp.float32)
    scale = ATTN_HEAD_DIM ** -0.5

    def one_chunk(c):
        start = c * CHUNK
        q_c = lax.dynamic_slice_in_dim(q, start, CHUNK, axis=1)
        k_b = lax.dynamic_slice_in_dim(kpad, start, BAND * CHUNK, axis=1)
        v_b = lax.dynamic_slice_in_dim(vpad, start, BAND * CHUNK, axis=1)
        s = jnp.einsum('bqhd,bkhd->bhqk', q_c, k_b,
                       preferred_element_type=jnp.float32) * scale + bias[None]
        k_abs = start - LEFT_CHUNKS * CHUNK + jnp.arange(BAND * CHUNK)
        s = jnp.where((k_abs >= 0)[None, None, None, :], s, -jnp.inf)
        p = jax.nn.softmax(s, axis=-1).astype(v.dtype)
        return jnp.einsum('bhqk,bkhd->bqhd', p, v_b)

    o = lax.map(one_chunk, jnp.arange(nc))
    o = jnp.moveaxis(o, 0, 1).reshape(B, S, D_MODEL)
    return o @ w_o


def causal_conv(x, w):
    S = x.shape[1]
    xp = jnp.pad(x, ((0, 0), (CONV_K - 1, 0), (0, 0)))
    y = xp[:, 0:S] * w[0]
    for j in range(1, CONV_K):
        y = y + xp[:, j:j + S] * w[j]
    return y


def chunk_gated_delta_rule(q, k, v, g, beta):
    B, S, H, dk = q.shape
    dv = v.shape[-1]
    nc = S // CHUNK

    def to_chunks(t):
        t = t.astype(jnp.float32).reshape((B, nc, CHUNK, H) + t.shape[3:])
        return jnp.moveaxis(t, 3, 1)

    q, k, v, g, beta = map(to_chunks, (q, k, v, g, beta))
    gc = jnp.cumsum(g, axis=-1)
    idx = jnp.arange(CHUNK)
    incl = idx[:, None] >= idx[None, :]
    strict = idx[:, None] > idx[None, :]
    decay = jnp.exp(jnp.where(incl, gc[..., :, None] - gc[..., None, :], -jnp.inf))
    kb = k * beta[..., None]
    vb = v * beta[..., None]
    L = jnp.where(strict, jnp.einsum('bhnid,bhnjd->bhnij', kb, k) * decay, 0.0)
    M = L + jnp.eye(CHUNK, dtype=jnp.float32)
    rhs = jnp.concatenate([vb, kb * jnp.exp(gc)[..., None]], axis=-1)
    sol = lax.linalg.triangular_solve(M, rhs, left_side=True, lower=True,
                                      unit_diagonal=True)
    u, w = sol[..., :dv], sol[..., dv:]
    qk = jnp.einsum('bhnid,bhnjd->bhnij', q, k) * decay
    q_dec = q * jnp.exp(gc)[..., None]
    k_dec = k * jnp.exp(gc[..., -1:] - gc)[..., None]
    g_last = jnp.exp(gc[..., -1])
    xs = tuple(jnp.moveaxis(t, 2, 0) for t in (u, w, qk, q_dec, k_dec, g_last))

    def step(state, inp):
        u_c, w_c, qk_c, qd_c, kd_c, gl_c = inp
        v_new = u_c - jnp.einsum('bhck,bhkv->bhcv', w_c, state)
        o_c = jnp.einsum('bhck,bhkv->bhcv', qd_c, state) + jnp.einsum('bhij,bhjv->bhiv', qk_c, v_new)
        state = state * gl_c[..., None, None] + jnp.einsum('bhck,bhcv->bhkv', kd_c, v_new)
        return state, o_c

    s0 = jnp.zeros((B, H, dk, dv), jnp.float32)
    _, o = lax.scan(step, s0, xs)
    o = jnp.transpose(o, (1, 0, 3, 2, 4))
    return o.reshape(B, S, H, dv)


def gated_deltanet(h, w_in, conv_w, a_log, dt_bias, head_norm, w_o):
    B, S, _ = h.shape
    proj = h @ w_in
    qkv = proj[..., :3 * DN_WIDTH]
    z = proj[..., 3 * DN_WIDTH:4 * DN_WIDTH]
    a = proj[..., 4 * DN_WIDTH:4 * DN_WIDTH + DN_HEADS]
    b = proj[..., 4 * DN_WIDTH + DN_HEADS:]
    qkv = jax.nn.silu(causal_conv(qkv, conv_w))
    q, k, v = jnp.split(qkv, 3, axis=-1)
    q = l2norm(q.reshape(B, S, DN_HEADS, DN_HEAD_DIM)) * (DN_HEAD_DIM ** -0.5)
    k = l2norm(k.reshape(B, S, DN_HEADS, DN_HEAD_DIM))
    v = v.reshape(B, S, DN_HEADS, DN_HEAD_DIM)
    beta = jax.nn.sigmoid(b.astype(jnp.float32))
    g = -jnp.exp(a_log.astype(jnp.float32)) * jax.nn.softplus(
        a.astype(jnp.float32) + dt_bias.astype(jnp.float32))
    o = chunk_gated_delta_rule(q, k, v, g, beta)
    o = rmsnorm(o, head_norm) * jax.nn.silu(
        z.astype(jnp.float32).reshape(B, S, DN_HEADS, DN_HEAD_DIM))
    return o.reshape(B, S, DN_WIDTH).astype(h.dtype) @ w_o


def squared_relu_mlp(h, w_up, w_down):
    return jnp.square(jax.nn.relu(h @ w_up)) @ w_down


def setup_inputs(seed: int = 0) -> dict:
    key = jax.random.key(seed)
    ks = jax.random.split(key, 18)
    f32 = jnp.float32

    def nrm(k, shape, fan_in, gain=1.0):
        return jax.random.normal(k, shape, f32) * (gain * fan_in ** -0.5)

    def gains(k, shape):
        return 1.0 + 0.02 * jax.random.normal(k, shape, f32)

    NA, NB = N_ATTN_LAYERS, N_DN_LAYERS
    x = jax.random.normal(ks[0], (BATCH, SEQ, D_MODEL), f32)
    attn_norm = gains(ks[1], (NA, D_MODEL))
    attn_w_qkv = nrm(ks[2], (NA, D_MODEL, 3 * D_MODEL), D_MODEL)
    attn_rel_bias = 0.1 * jax.random.normal(ks[3], (NA, ATTN_HEADS, 2 * MAX_REL + 1), f32)
    attn_w_o = nrm(ks[4], (NA, D_MODEL, D_MODEL), D_MODEL)
    dn_norm = gains(ks[5], (NB, D_MODEL))
    dn_w_in = nrm(ks[6], (NB, D_MODEL, 4 * DN_WIDTH + 2 * DN_HEADS), D_MODEL)
    dn_conv_w = nrm(ks[7], (NB, CONV_K, 3 * DN_WIDTH), CONV_K)
    dn_a_log = jnp.log(jax.random.uniform(ks[8], (NB, DN_HEADS), f32, 1.0, 16.0))
    dt = jnp.exp(jax.random.uniform(ks[9], (NB, DN_HEADS), f32,
                                    math.log(1e-3), math.log(1e-1)))
    dn_dt_bias = dt + jnp.log(-jnp.expm1(-dt))
    dn_head_norm = gains(ks[10], (NB, DN_HEAD_DIM))
    dn_w_o = nrm(ks[11], (NB, DN_WIDTH, D_MODEL), DN_WIDTH)
    mlp_norm = gains(ks[12], (DEPTH, D_MODEL))
    mlp_w_up = nrm(ks[13], (DEPTH, D_MODEL, D_FF), D_MODEL)
    mlp_w_down = nrm(ks[14], (DEPTH, D_FF, D_MODEL), D_FF, gain=0.5)
    final_norm = gains(ks[15], (D_MODEL,))
    return {"x": x, "attn_norm": attn_norm, "attn_w_qkv": attn_w_qkv,
            "attn_rel_bias": attn_rel_bias, "attn_w_o": attn_w_o,
            "dn_norm": dn_norm, "dn_w_in": dn_w_in, "dn_conv_w": dn_conv_w,
            "dn_a_log": dn_a_log, "dn_dt_bias": dn_dt_bias, "dn_head_norm": dn_head_norm,
            "dn_w_o": dn_w_o, "mlp_norm": mlp_norm, "mlp_w_up": mlp_w_up,
            "mlp_w_down": mlp_w_down, "final_norm": final_norm}


def reference(x, attn_norm, attn_w_qkv, attn_rel_bias, attn_w_o,
              dn_norm, dn_w_in, dn_conv_w, dn_a_log, dn_dt_bias, dn_head_norm, dn_w_o,
              mlp_norm, mlp_w_up, mlp_w_down, final_norm):
    for i in range(DEPTH):
        j = i // N_MIXERS
        if i % N_MIXERS == 0:
            x = x + chunked_rel_attention(rmsnorm(x, attn_norm[j]), attn_w_qkv[j],
                                          attn_rel_bias[j], attn_w_o[j])
        else:
            x = x + gated_deltanet(rmsnorm(x, dn_norm[j]), dn_w_in[j], dn_conv_w[j],
                                   dn_a_log[j], dn_dt_bias[j], dn_head_norm[j], dn_w_o[j])
        x = x + squared_relu_mlp(rmsnorm(x, mlp_norm[i]), mlp_w_up[i], mlp_w_down[i])
    return rmsnorm(x, final_norm)
```

```python
import functools

import jax
import jax.numpy as jnp
from jax import lax
from jax.experimental import pallas as pl
from jax.experimental.pallas import tpu as pltpu

D_MODEL = 1024
CHUNK = 64
LEFT_CHUNKS = 8
ATTN_HEAD_DIM = 64
ATTN_HEADS = D_MODEL // ATTN_HEAD_DIM
MAX_REL = 256
DN_HEAD_DIM = 128
DN_HEADS = D_MODEL // DN_HEAD_DIM
CONV_K = 4
D_FF = 4 * D_MODEL
EPS = 1e-6

LANES = 128
SUBLANES = 8
VMEM_LIMIT = 56 * 1024 * 1024

ROW_TILE = 512
FF_TILE = 1024
ATTN_Q = 2 * CHUNK
ATTN_KV = ATTN_Q + LEFT_CHUNKS * CHUNK
ATTN_PAD = LEFT_CHUNKS * CHUNK
DN_STEP_CHUNKS = 2
DN_STEP = DN_STEP_CHUNKS * CHUNK

F32 = jnp.float32
BF16 = jnp.bfloat16
NEG_INF = float("-inf")


def _rms(x, g):
    ms = jnp.mean(x * x, axis=-1, keepdims=True)
    return (x * lax.rsqrt(ms + EPS)) * g


def _dot(a, b):
    return jnp.dot(a, b, preferred_element_type=F32)


def _dot_nt(a, b):
    return lax.dot_general(a, b, (((1,), (1,)), ((), ())), preferred_element_type=F32)


def _dot_tn(a, b):
    return lax.dot_general(a, b, (((0,), (0,)), ((), ())), preferred_element_type=F32)


def _resident(shape):
    return pl.BlockSpec(shape, lambda *_: (0,) * len(shape), pipeline_mode=pl.Buffered(1))


def _params(*sem):
    return pltpu.CompilerParams(dimension_semantics=sem, vmem_limit_bytes=VMEM_LIMIT)


def _qkv_kernel(x_ref, g_ref, w_ref, o_ref):
    xn = _rms(x_ref[...], g_ref[...]).astype(BF16)
    for j in range(3):
        cols = slice(j * D_MODEL, (j + 1) * D_MODEL)
        r = _dot(xn, w_ref[:, cols])
        if j == 0:
            r = r * (ATTN_HEAD_DIM ** -0.5)
        o_ref[:, cols] = r.astype(BF16)


def _qkv_proj(x2, g, w):
    t = x2.shape[0]
    return pl.pallas_call(
        _qkv_kernel,
        grid=(t // ROW_TILE,),
        in_specs=[pl.BlockSpec((ROW_TILE, D_MODEL), lambda i: (i, 0)),
                  _resident((1, D_MODEL)),
                  _resident((D_MODEL, 3 * D_MODEL))],
        out_specs=pl.BlockSpec((ROW_TILE, 3 * D_MODEL), lambda i: (i, 0)),
        out_shape=jax.ShapeDtypeStruct((t, 3 * D_MODEL), BF16),
        compiler_params=_params("arbitrary"),
        name="qkv_proj",
    )(x2, g, w)


def _attn_kernel(q_ref, k_ref, v_ref, bias_ref, o_ref, kpad_ref, vpad_ref):
    seq = q_ref.shape[0]
    zeros = jnp.zeros((ATTN_PAD, LANES), BF16)
    kpad_ref[0:ATTN_PAD, :] = zeros
    vpad_ref[0:ATTN_PAD, :] = zeros
    kpad_ref[ATTN_PAD:, :] = k_ref[...]
    vpad_ref[ATTN_PAD:, :] = v_ref[...]
    first_head = lax.broadcasted_iota(jnp.int32, (ATTN_Q, LANES), 1) < ATTN_HEAD_DIM
    key_pos = lax.broadcasted_iota(jnp.int32, (2 * ATTN_Q, ATTN_KV), 1)

    def step(cp, masked):
        r0 = pl.multiple_of(cp * ATTN_Q, ATTN_Q)
        q2 = q_ref[pl.ds(r0, ATTN_Q), :]
        zq = jnp.zeros_like(q2)
        qs = jnp.concatenate([jnp.where(first_head, q2, zq), jnp.where(first_head, zq, q2)], axis=0)
        kw = kpad_ref[pl.ds(r0, ATTN_KV), :]
        vw = vpad_ref[pl.ds(r0, ATTN_KV), :]
        s = _dot_nt(qs, kw) + bias_ref[...]
        if masked:
            s = jnp.where(key_pos >= ATTN_PAD - r0, s, NEG_INF)
        m = jnp.max(s, axis=-1, keepdims=True)
        p = jnp.exp(s - m)
        l = jnp.sum(p, axis=-1, keepdims=True)
        o = _dot(p.astype(BF16), vw) / l
        o_ref[pl.ds(r0, ATTN_Q), :] = jnp.where(first_head, o[:ATTN_Q], o[ATTN_Q:]).astype(BF16)

    n_masked = ATTN_PAD // ATTN_Q

    def masked_body(cp, c):
        step(cp, True)
        return c

    def plain_body(cp, c):
        step(cp, False)
        return c

    lax.fori_loop(0, n_masked, masked_body, 0)
    lax.fori_loop(n_masked, seq // ATTN_Q, plain_body, 0)


def _attention(qkv, bias):
    b, s, _ = qkv.shape
    pairs = ATTN_HEADS // 2

    def col(off):
        return pl.BlockSpec((None, s, LANES), lambda bi, hp: (bi, 0, off + hp))

    return pl.pallas_call(
        _attn_kernel,
        grid=(b, pairs),
        in_specs=[col(0), col(pairs), col(2 * pairs),
                  pl.BlockSpec((None, 2 * ATTN_Q, ATTN_KV), lambda bi, hp: (hp, 0, 0))],
        out_specs=pl.BlockSpec((None, s, LANES), lambda bi, hp: (bi, 0, hp)),
        out_shape=jax.ShapeDtypeStruct((b, s, D_MODEL), BF16),
        scratch_shapes=[pltpu.VMEM((s + ATTN_PAD, LANES), BF16),
                        pltpu.VMEM((s + ATTN_PAD, LANES), BF16)],
        compiler_params=_params("arbitrary", "arbitrary"),
        name="band_attention",
    )(qkv, qkv, qkv, bias)


def _attn_bias_table(rel_bias):
    qq = jnp.arange(ATTN_Q)[:, None]
    kb = jnp.arange(ATTN_KV)[None, :]
    idx = jnp.clip(qq + ATTN_PAD - kb, -MAX_REL, MAX_REL) + MAX_REL
    lag = kb // CHUNK - qq // CHUNK
    visible = (lag >= 0) & (lag <= LEFT_CHUNKS)
    table = jnp.where(visible[None], rel_bias[:, idx].astype(F32), NEG_INF)
    return table.reshape(ATTN_HEADS // 2, 2 * ATTN_Q, ATTN_KV)


def _proj_mlp_kernel(*refs, final):
    if final:
        x_ref, a_ref, wo_ref, g_ref, wup_ref, wdn_ref, fg_ref, o_ref, act_ref = refs
    else:
        x_ref, a_ref, wo_ref, g_ref, wup_ref, wdn_ref, o_ref, act_ref = refs
    o_ref[...] = x_ref[...] + _dot(a_ref[...], wo_ref[...])
    hn = _rms(o_ref[...], g_ref[...]).astype(BF16)
    for f in range(D_FF // FF_TILE):
        cols = slice(f * FF_TILE, (f + 1) * FF_TILE)
        u = jnp.maximum(_dot(hn, wup_ref[:, cols]), 0.0)
        act_ref[:, cols] = (u * u).astype(BF16)
    y = o_ref[...] + _dot(act_ref[...], wdn_ref[...])
    if final:
        y = _rms(y, fg_ref[...])
    o_ref[...] = y


def _proj_mlp(x2, a2, wo, g, wup, wdn, final_g=None):
    t = x2.shape[0]
    final = final_g is not None
    in_specs = [pl.BlockSpec((ROW_TILE, D_MODEL), lambda i: (i, 0)),
                pl.BlockSpec((ROW_TILE, D_MODEL), lambda i: (i, 0)),
                _resident((D_MODEL, D_MODEL)),
                _resident((1, D_MODEL)),
                _resident((D_MODEL, D_FF)),
                _resident((D_FF, D_MODEL))]
    args = [x2, a2, wo, g, wup, wdn]
    if final:
        in_specs.append(_resident((1, D_MODEL)))
        args.append(final_g)
    return pl.pallas_call(
        functools.partial(_proj_mlp_kernel, final=final),
        grid=(t // ROW_TILE,),
        in_specs=in_specs,
        out_specs=pl.BlockSpec((ROW_TILE, D_MODEL), lambda i: (i, 0)),
        out_shape=jax.ShapeDtypeStruct((t, D_MODEL), F32),
        scratch_shapes=[pltpu.VMEM((ROW_TILE, D_FF), BF16)],
        compiler_params=_params("arbitrary"),
        name="proj_mlp_final" if final else "proj_mlp",
    )(*args)


def _dn_proj_kernel(x_ref, g_ref, w_ref, wab_ref, wabt_ref, cw_ref,
                    q_ref, k_ref, v_ref, z_ref, ab_ref, abt_ref, buf_ref, *, tiles_per_seq):
    i = pl.program_id(0)
    xn = _rms(x_ref[...], g_ref[...]).astype(BF16)
    rows = x_ref.shape[0]
    halo = CONV_K - 1

    @pl.when(i % tiles_per_seq == 0)
    def _():
        buf_ref[:, 0:SUBLANES, :] = jnp.zeros((3, SUBLANES, D_MODEL), F32)

    outs = (q_ref, k_ref, v_ref)
    for j in range(3):
        cols = slice(j * D_MODEL, (j + 1) * D_MODEL)
        buf_ref[j, SUBLANES:SUBLANES + rows, :] = _dot(xn, w_ref[:, cols])
        y = None
        for tap in range(CONV_K):
            start = SUBLANES - halo + tap
            term = buf_ref[j, start:start + rows, :] * cw_ref[tap:tap + 1, cols]
            y = term if y is None else y + term
        buf_ref[j, SUBLANES - halo:SUBLANES, :] = buf_ref[j, SUBLANES + rows - halo:SUBLANES + rows, :]
        y = y * jax.nn.sigmoid(y)
        if j < 2:
            for h in range(DN_HEADS):
                hs = slice(h * DN_HEAD_DIM, (h + 1) * DN_HEAD_DIM)
                yh = y[:, hs]
                yh = yh * lax.rsqrt(jnp.sum(yh * yh, axis=-1, keepdims=True) + EPS)
                if j == 0:
                    yh = yh * (DN_HEAD_DIM ** -0.5)
                outs[j][:, hs] = yh.astype(BF16)
        else:
            outs[j][...] = y.astype(BF16)
    z_ref[...] = _dot(xn, w_ref[:, 3 * D_MODEL:4 * D_MODEL]).astype(BF16)
    ab_ref[...] = _dot(xn, wab_ref[...])
    abt_ref[...] = _dot_nt(wabt_ref[...], xn)


def _dn_proj(x2, g, w_main, w_ab, w_abt, conv_w, seq):
    t = x2.shape[0]
    row_spec = pl.BlockSpec((ROW_TILE, D_MODEL), lambda i: (i, 0))
    act = jax.ShapeDtypeStruct((t, D_MODEL), BF16)
    return pl.pallas_call(
        functools.partial(_dn_proj_kernel, tiles_per_seq=seq // ROW_TILE),
        grid=(t // ROW_TILE,),
        in_specs=[row_spec,
                  _resident((1, D_MODEL)),
                  _resident((D_MODEL, 4 * D_MODEL)),
                  _resident((D_MODEL, LANES)),
                  _resident((2 * DN_HEADS, D_MODEL)),
                  _resident((CONV_K, 3 * D_MODEL))],
        out_specs=[row_spec, row_spec, row_spec, row_spec,
                   pl.BlockSpec((ROW_TILE, LANES), lambda i: (i, 0)),
                   pl.BlockSpec((2 * DN_HEADS, ROW_TILE), lambda i: (0, i))],
        out_shape=[act, act, act, act,
                   jax.ShapeDtypeStruct((t, LANES), F32),
                   jax.ShapeDtypeStruct((2 * DN_HEADS, t), F32)],
        scratch_shapes=[pltpu.VMEM((3, SUBLANES + ROW_TILE, D_MODEL), F32)],
        compiler_params=_params("arbitrary"),
        name="dn_proj",
    )(x2, g, w_main, w_ab, w_abt, conv_w)


def _split3(a):
    a1 = a.astype(BF16)
    r = a - a1.astype(F32)
    a2 = r.astype(BF16)
    a3 = (r - a2.astype(F32)).astype(BF16)
    return a1, a2, a3


def _softplus(x):
    return jnp.maximum(x, 0.0) + jnp.log1p(jnp.exp(-jnp.abs(x)))


def _delta_kernel(q_ref, k_ref, v_ref, z_ref, ab_ref, abt_ref, prow_ref, pcol_ref, hn_ref,
                  o_ref, state_ref):
    @pl.when(pl.program_id(1) == 0)
    def _():
        state_ref[...] = jnp.zeros_like(state_ref)

    ab = ab_ref[...]
    g_tok = -jnp.exp(prow_ref[0:1, :]) * _softplus(ab + prow_ref[1:2, :])
    beta_tok = jax.nn.sigmoid(ab)
    g_head = -jnp.exp(pcol_ref[:, 0:1]) * _softplus(abt_ref[0:DN_HEADS, :] + pcol_ref[:, 1:2])
    g_head3 = _split3(g_head)

    ci = lax.broadcasted_iota(jnp.int32, (CHUNK, CHUNK), 0)
    cj = lax.broadcasted_iota(jnp.int32, (CHUNK, CHUNK), 1)
    incl = ci >= cj
    strict = ci > cj
    tri = jnp.where(incl, 1.0, 0.0).astype(BF16)
    ti = lax.broadcasted_iota(jnp.int32, (DN_STEP, CHUNK), 0)
    tj = lax.broadcasted_iota(jnp.int32, (DN_STEP, CHUNK), 1)

    for n in range(DN_STEP_CHUNKS):
        rows = slice(n * CHUNK, (n + 1) * CHUNK)
        gc_tok = sum(_dot(tri, piece) for piece in _split3(g_tok[rows]))
        sel = jnp.where((ti >= n * CHUNK) & (ti - n * CHUNK <= tj), 1.0, 0.0).astype(BF16)
        gc_head = sum(_dot(piece, sel) for piece in g_head3)
        g_last = jnp.broadcast_to(gc_tok[CHUNK - 1:CHUNK, :], (DN_HEAD_DIM, LANES))

        for h in range(DN_HEADS):
            hs = slice(h * DN_HEAD_DIM, (h + 1) * DN_HEAD_DIM)
            gcol = gc_tok[:, h:h + 1]
            grow = gc_head[h:h + 1, :]
            gl = g_last[:, h:h + 1]
            decay = jnp.exp(jnp.where(incl, gcol - grow, NEG_INF))
            beta = beta_tok[rows, DN_HEADS + h:DN_HEADS + h + 1]
            egc = jnp.exp(gcol)
            q_b = q_ref[rows, hs]
            k_b = k_ref[rows, hs]
            k_f = k_b.astype(F32)
            kb = k_f * beta
            vb = v_ref[rows, hs].astype(F32) * beta

            lmat = jnp.where(strict, _dot_nt(kb.astype(BF16), k_b) * decay, 0.0)
            sol = jnp.concatenate([vb, kb * egc], axis=-1)
            pw = lmat.astype(BF16)
            sol = sol - _dot(pw, sol.astype(BF16))
            for _ in range(5):
                pw = _dot(pw, pw).astype(BF16)
                sol = sol + _dot(pw, sol.astype(BF16))
            u = sol[:, :DN_HEAD_DIM]
            w = sol[:, DN_HEAD_DIM:]

            qk = _dot_nt(q_b, k_b) * decay
            q_dec = q_b.astype(F32) * egc
            k_dec = k_f * jnp.exp(gl[:CHUNK] - gcol)

            state = state_ref[h]
            state_b = state.astype(BF16)
            v_new = u - _dot(w.astype(BF16), state_b)
            v_new_b = v_new.astype(BF16)
            o = _dot(q_dec.astype(BF16), state_b) + _dot(qk.astype(BF16), v_new_b)
            state_ref[h] = state * jnp.exp(gl) + _dot_tn(k_dec.astype(BF16), v_new_b)

            o = _rms(o, hn_ref[...])
            z = z_ref[rows, hs].astype(F32)
            o_ref[rows, hs] = (o * (z * jax.nn.sigmoid(z))).astype(BF16)


def _delta_rule(q, k, v, z, ab, abt, prow, pcol, hn, batch, seq):
    steps = seq // DN_STEP
    row_spec = pl.BlockSpec((DN_STEP, D_MODEL), lambda b, c: (b * steps + c, 0))
    return pl.pallas_call(
        _delta_kernel,
        grid=(batch, steps),
        in_specs=[row_spec, row_spec, row_spec, row_spec,
                  pl.BlockSpec((DN_STEP, LANES), lambda b, c: (b * steps + c, 0)),
                  pl.BlockSpec((2 * DN_HEADS, DN_STEP), lambda b, c: (0, b * steps + c)),
                  _resident((SUBLANES, LANES)),
                  _resident((DN_HEADS, 2)),
                  _resident((1, DN_HEAD_DIM))],
        out_specs=row_spec,
        out_shape=jax.ShapeDtypeStruct((batch * seq, D_MODEL), BF16),
        scratch_shapes=[pltpu.VMEM((DN_HEADS, DN_HEAD_DIM, DN_HEAD_DIM), F32)],
        compiler_params=_params("arbitrary", "arbitrary"),
        name="delta_rule",
    )(q, k, v, z, ab, abt, prow, pcol, hn)


def kernel(x, attn_norm, attn_w_qkv, attn_rel_bias, attn_w_o, dn_norm, dn_w_in, dn_conv_w,
           dn_a_log, dn_dt_bias, dn_head_norm, dn_w_o, mlp_norm, mlp_w_up, mlp_w_down, final_norm):
    batch, seq, d = x.shape
    assert d == D_MODEL and seq % ROW_TILE == 0 and seq % DN_STEP == 0 and seq % ATTN_Q == 0
    assert attn_norm.shape[0] == 1 and dn_norm.shape[0] == 1 and mlp_norm.shape[0] == 2
    t = batch * seq
    x2 = x.reshape(t, d)

    qkv = _qkv_proj(x2, attn_norm[0][None], attn_w_qkv[0].astype(BF16))
    att = _attention(qkv.reshape(batch, seq, 3 * d), _attn_bias_table(attn_rel_bias[0]))
    x2 = _proj_mlp(x2, att.reshape(t, d), attn_w_o[0].astype(BF16), mlp_norm[0][None],
                   mlp_w_up[0].astype(BF16), mlp_w_down[0].astype(BF16))

    w_in = dn_w_in[0]
    w_ab = w_in[:, 4 * d:]
    w_ab_pad = jnp.pad(w_ab, ((0, 0), (0, LANES - 2 * DN_HEADS))).astype(BF16)
    q, k, v, z, ab, abt = _dn_proj(x2, dn_norm[0][None], w_in[:, :4 * d].astype(BF16), w_ab_pad,
                                   w_ab.T.astype(BF16), dn_conv_w[0], seq)
    prow = jnp.zeros((SUBLANES, LANES), F32)
    prow = prow.at[0, :DN_HEADS].set(dn_a_log[0]).at[1, :DN_HEADS].set(dn_dt_bias[0])
    pcol = jnp.stack([dn_a_log[0], dn_dt_bias[0]], axis=1)
    og = _delta_rule(q, k, v, z, ab, abt, prow, pcol, dn_head_norm[0][None], batch, seq)
    x2 = _proj_mlp(x2, og, dn_w_o[0].astype(BF16), mlp_norm[1][None],
                   mlp_w_up[1].astype(BF16), mlp_w_down[1].astype(BF16), final_g=final_norm[None])
    return x2.reshape(batch, seq, d)
```

```python
import functools

import jax
import jax.numpy as jnp
from jax import lax
from jax.experimental import pallas as pl
from jax.experimental.pallas import tpu as pltpu

D_MODEL = 1024
CHUNK = 64
LEFT_CHUNKS = 8
ATTN_HEAD_DIM = 64
ATTN_HEADS = D_MODEL // ATTN_HEAD_DIM
MAX_REL = 256
DN_HEAD_DIM = 128
DN_HEADS = D_MODEL // DN_HEAD_DIM
CONV_K = 4
D_FF = 4 * D_MODEL
EPS = 1e-6

LANES = 128
SUBLANES = 8
VMEM_LIMIT = 56 * 1024 * 1024

ROW_TILE = 512
FF_TILE = 1024
ATTN_Q = 2 * CHUNK
ATTN_KV = ATTN_Q + LEFT_CHUNKS * CHUNK
ATTN_PAD = LEFT_CHUNKS * CHUNK
ATTN_REL = ATTN_KV + ATTN_Q
DN_GATE_BLOCK = 2 * CHUNK

F32 = jnp.float32
BF16 = jnp.bfloat16
NEG_INF = float("-inf")


def _rms(x, g):
    ms = jnp.mean(x * x, axis=-1, keepdims=True)
    return (x * lax.rsqrt(ms + EPS)) * g


def _dot(a, b):
    return jnp.dot(a, b, preferred_element_type=F32)


def _dot_nt(a, b):
    return lax.dot_general(a, b, (((1,), (1,)), ((), ())), preferred_element_type=F32)


def _dot_tn(a, b):
    return lax.dot_general(a, b, (((0,), (0,)), ((), ())), preferred_element_type=F32)


def _resident(shape):
    return pl.BlockSpec(shape, lambda *_: (0,) * len(shape), pipeline_mode=pl.Buffered(1))


def _params(*sem):
    return pltpu.CompilerParams(dimension_semantics=sem, vmem_limit_bytes=VMEM_LIMIT)


def _qkv_kernel(x_ref, g_ref, w_ref, o_ref):
    xn = _rms(x_ref[...], g_ref[...]).astype(BF16)
    for j in range(3):
        cols = slice(j * D_MODEL, (j + 1) * D_MODEL)
        r = _dot(xn, w_ref[:, cols])
        if j == 0:
            r = r * (ATTN_HEAD_DIM ** -0.5)
        o_ref[:, cols] = r.astype(BF16)


def _qkv_proj(x2, g, w):
    t = x2.shape[0]
    return pl.pallas_call(
        _qkv_kernel,
        grid=(t // ROW_TILE,),
        in_specs=[pl.BlockSpec((ROW_TILE, D_MODEL), lambda i: (i, 0)),
                  _resident((1, D_MODEL)),
                  _resident((D_MODEL, 3 * D_MODEL))],
        out_specs=pl.BlockSpec((ROW_TILE, 3 * D_MODEL), lambda i: (i, 0)),
        out_shape=jax.ShapeDtypeStruct((t, 3 * D_MODEL), BF16),
        compiler_params=_params("arbitrary"),
        name="qkv_proj",
    )(x2, g, w)


def _attn_kernel(q_ref, k_ref, v_ref, rel_ref, o_ref, kpad_ref, vpad_ref, bias_ref):
    seq = q_ref.shape[0]

    @pl.when(pl.program_id(1) == 0)
    def _():
        qq = lax.broadcasted_iota(jnp.int32, (ATTN_Q, ATTN_KV), 0)
        kb = lax.broadcasted_iota(jnp.int32, (ATTN_Q, ATTN_KV), 1)
        lag = kb // CHUNK - qq // CHUNK
        visible = (lag >= 0) & (lag <= LEFT_CHUNKS)
        for e in range(2):
            row = jnp.broadcast_to(rel_ref[e:e + 1, :], (ATTN_Q, ATTN_REL))
            skew = pltpu.roll(row, 0, 1, stride=1, stride_axis=0)
            bias_ref[e * ATTN_Q:(e + 1) * ATTN_Q, :] = jnp.where(visible, skew[:, :ATTN_KV], NEG_INF)

    zeros = jnp.zeros((ATTN_PAD, LANES), BF16)
    kpad_ref[0:ATTN_PAD, :] = zeros
    vpad_ref[0:ATTN_PAD, :] = zeros
    kpad_ref[ATTN_PAD:, :] = k_ref[...]
    vpad_ref[ATTN_PAD:, :] = v_ref[...]
    first_head = lax.broadcasted_iota(jnp.int32, (ATTN_Q, LANES), 1) < ATTN_HEAD_DIM
    key_pos = lax.broadcasted_iota(jnp.int32, (2 * ATTN_Q, ATTN_KV), 1)

    def step(cp, masked):
        r0 = pl.multiple_of(cp * ATTN_Q, ATTN_Q)
        q2 = q_ref[pl.ds(r0, ATTN_Q), :]
        zq = jnp.zeros_like(q2)
        qs = jnp.concatenate([jnp.where(first_head, q2, zq), jnp.where(first_head, zq, q2)], axis=0)
        kw = kpad_ref[pl.ds(r0, ATTN_KV), :]
        vw = vpad_ref[pl.ds(r0, ATTN_KV), :]
        s = _dot_nt(qs, kw) + bias_ref[...]
        if masked:
            s = jnp.where(key_pos >= ATTN_PAD - r0, s, NEG_INF)
        m = jnp.max(s, axis=-1, keepdims=True)
        p = jnp.exp(s - m)
        l = jnp.sum(p, axis=-1, keepdims=True)
        o = _dot(p.astype(BF16), vw) / l
        o_ref[pl.ds(r0, ATTN_Q), :] = jnp.where(first_head, o[:ATTN_Q], o[ATTN_Q:]).astype(BF16)

    n_masked = ATTN_PAD // ATTN_Q

    def masked_body(cp, c):
        step(cp, True)
        return c

    def plain_body(cp, c):
        step(cp, False)
        return c

    lax.fori_loop(0, n_masked, masked_body, 0)
    lax.fori_loop(n_masked, seq // ATTN_Q, plain_body, 0)


def _attention(qkv, rel):
    b, s, _ = qkv.shape
    pairs = ATTN_HEADS // 2

    def col(off):
        return pl.BlockSpec((None, s, LANES), lambda hp, bi: (bi, 0, off + hp))

    return pl.pallas_call(
        _attn_kernel,
        grid=(pairs, b),
        in_specs=[col(0), col(pairs), col(2 * pairs),
                  pl.BlockSpec((None, 2, ATTN_REL), lambda hp, bi: (hp, 0, 0))],
        out_specs=pl.BlockSpec((None, s, LANES), lambda hp, bi: (bi, 0, hp)),
        out_shape=jax.ShapeDtypeStruct((b, s, D_MODEL), BF16),
        scratch_shapes=[pltpu.VMEM((s + ATTN_PAD, LANES), BF16),
                        pltpu.VMEM((s + ATTN_PAD, LANES), BF16),
                        pltpu.VMEM((2 * ATTN_Q, ATTN_KV), F32)],
        compiler_params=_params("arbitrary", "arbitrary"),
        name="band_attention",
    )(qkv, qkv, qkv, rel)


def _attn_rel_rows(rel_bias):
    far = rel_bias[:, 2 * MAX_REL:]
    near = rel_bias[:, 2 * MAX_REL - 1:MAX_REL - ATTN_Q:-1]
    n_far = ATTN_PAD - MAX_REL + 1
    rows = jnp.concatenate(
        [jnp.broadcast_to(far, (ATTN_HEADS, n_far)), near,
         jnp.broadcast_to(far, (ATTN_HEADS, ATTN_REL - n_far - near.shape[1]))], axis=1)
    return rows.astype(F32).reshape(ATTN_HEADS // 2, 2, ATTN_REL)


def _proj_mlp_kernel(*refs, final):
    if final:
        x_ref, a_ref, wo_ref, g_ref, wup_ref, wdn_ref, fg_ref, o_ref, act_ref = refs
    else:
        x_ref, a_ref, wo_ref, g_ref, wup_ref, wdn_ref, o_ref, act_ref = refs
    o_ref[...] = x_ref[...] + _dot(a_ref[...], wo_ref[...])
    hn = _rms(o_ref[...], g_ref[...]).astype(BF16)
    for f in range(D_FF // FF_TILE):
        cols = slice(f * FF_TILE, (f + 1) * FF_TILE)
        u = jnp.maximum(_dot(hn, wup_ref[:, cols]), 0.0)
        act_ref[:, cols] = (u * u).astype(BF16)
    y = o_ref[...] + _dot(act_ref[...], wdn_ref[...])
    if final:
        y = _rms(y, fg_ref[...])
    o_ref[...] = y


def _proj_mlp(x2, a2, wo, g, wup, wdn, final_g=None):
    t = x2.shape[0]
    final = final_g is not None
    in_specs = [pl.BlockSpec((ROW_TILE, D_MODEL), lambda i: (i, 0)),
                pl.BlockSpec((ROW_TILE, D_MODEL), lambda i: (i, 0)),
                _resident((D_MODEL, D_MODEL)),
                _resident((1, D_MODEL)),
                _resident((D_MODEL, D_FF)),
                _resident((D_FF, D_MODEL))]
    args = [x2, a2, wo, g, wup, wdn]
    if final:
        in_specs.append(_resident((1, D_MODEL)))
        args.append(final_g)
    return pl.pallas_call(
        functools.partial(_proj_mlp_kernel, final=final),
        grid=(t // ROW_TILE,),
        in_specs=in_specs,
        out_specs=pl.BlockSpec((ROW_TILE, D_MODEL), lambda i: (i, 0)),
        out_shape=jax.ShapeDtypeStruct((t, D_MODEL), F32),
        scratch_shapes=[pltpu.VMEM((ROW_TILE, D_FF), BF16)],
        compiler_params=_params("arbitrary"),
        name="proj_mlp_final" if final else "proj_mlp",
    )(*args)


def _dn_proj_kernel(x_ref, g_ref, w_ref, wab_ref, wabt_ref, cw_ref,
                    q_ref, k_ref, v_ref, z_ref, ab_ref, abt_ref, buf_ref, *, tiles_per_seq):
    i = pl.program_id(0)
    xn = _rms(x_ref[...], g_ref[...]).astype(BF16)
    rows = x_ref.shape[0]
    halo = CONV_K - 1

    @pl.when(i % tiles_per_seq == 0)
    def _():
        buf_ref[:, 0:SUBLANES, :] = jnp.zeros((3, SUBLANES, D_MODEL), F32)

    outs = (q_ref, k_ref, v_ref)
    for j in range(3):
        cols = slice(j * D_MODEL, (j + 1) * D_MODEL)
        buf_ref[j, SUBLANES:SUBLANES + rows, :] = _dot(xn, w_ref[:, cols])
        y = None
        for tap in range(CONV_K):
            start = SUBLANES - halo + tap
            term = buf_ref[j, start:start + rows, :] * cw_ref[tap:tap + 1, cols]
            y = term if y is None else y + term
        buf_ref[j, SUBLANES - halo:SUBLANES, :] = buf_ref[j, SUBLANES + rows - halo:SUBLANES + rows, :]
        y = y * jax.nn.sigmoid(y)
        if j < 2:
            for h in range(DN_HEADS):
                hs = slice(h * DN_HEAD_DIM, (h + 1) * DN_HEAD_DIM)
                yh = y[:, hs]
                yh = yh * lax.rsqrt(jnp.sum(yh * yh, axis=-1, keepdims=True) + EPS)
                if j == 0:
                    yh = yh * (DN_HEAD_DIM ** -0.5)
                outs[j][:, hs] = yh.astype(BF16)
        else:
            outs[j][...] = y.astype(BF16)
    z_ref[...] = _dot(xn, w_ref[:, 3 * D_MODEL:4 * D_MODEL]).astype(BF16)
    ab_ref[...] = _dot(xn, wab_ref[...])
    abt_ref[...] = _dot_nt(wabt_ref[...], xn)


def _dn_proj(x2, g, w_main, w_ab, w_abt, conv_w, seq):
    t = x2.shape[0]
    tiles_per_seq = seq // ROW_TILE
    row_spec = pl.BlockSpec((ROW_TILE, D_MODEL), lambda i: (i, 0))
    act = jax.ShapeDtypeStruct((t, D_MODEL), BF16)
    return pl.pallas_call(
        functools.partial(_dn_proj_kernel, tiles_per_seq=tiles_per_seq),
        grid=(t // ROW_TILE,),
        in_specs=[row_spec,
                  _resident((1, D_MODEL)),
                  _resident((D_MODEL, 4 * D_MODEL)),
                  _resident((D_MODEL, LANES)),
                  _resident((2 * DN_HEADS, D_MODEL)),
                  _resident((CONV_K, 3 * D_MODEL))],
        out_specs=[row_spec, row_spec, row_spec, row_spec,
                   pl.BlockSpec((ROW_TILE, LANES), lambda i: (i, 0)),
                   pl.BlockSpec((None, 2 * DN_HEADS, ROW_TILE),
                                lambda i: (i // tiles_per_seq, 0, i % tiles_per_seq))],
        out_shape=[act, act, act, act,
                   jax.ShapeDtypeStruct((t, LANES), F32),
                   jax.ShapeDtypeStruct((t // seq, 2 * DN_HEADS, seq), F32)],
        scratch_shapes=[pltpu.VMEM((3, SUBLANES + ROW_TILE, D_MODEL), F32)],
        compiler_params=_params("arbitrary"),
        name="dn_proj",
    )(x2, g, w_main, w_ab, w_abt, conv_w)


def _split3(a):
    a1 = a.astype(BF16)
    r = a - a1.astype(F32)
    a2 = r.astype(BF16)
    a3 = (r - a2.astype(F32)).astype(BF16)
    return a1, a2, a3


def _softplus(x):
    return jnp.maximum(x, 0.0) + jnp.log1p(jnp.exp(-jnp.abs(x)))


def _delta_kernel(q_ref, k_ref, v_ref, z_ref, ab_ref, abt_ref, prow_ref, pcol_ref, hn_ref,
                  o_ref, state_ref):
    batch = q_ref.shape[0]
    c = pl.program_id(0)

    @pl.when(c == 0)
    def _():
        state_ref[...] = jnp.zeros_like(state_ref)

    ri = lax.broadcasted_iota(jnp.int32, (CHUNK, LANES), 0)
    li = lax.broadcasted_iota(jnp.int32, (CHUNK, LANES), 1)
    lj = li % CHUNK
    hi_half = li >= CHUNK
    incl2 = ri >= lj
    strict2 = ri > lj
    eye_hi = jnp.where(hi_half & (ri == lj), 1.0, 0.0)
    ci = lax.broadcasted_iota(jnp.int32, (CHUNK, CHUNK), 0)
    cj = lax.broadcasted_iota(jnp.int32, (CHUNK, CHUNK), 1)
    tri = jnp.where(ci >= cj, 1.0, 0.0).astype(BF16)
    ti = lax.broadcasted_iota(jnp.int32, (DN_GATE_BLOCK, LANES), 0) - (c % 2) * CHUNK
    tj = lax.broadcasted_iota(jnp.int32, (DN_GATE_BLOCK, LANES), 1) % CHUNK
    in_chunk = (ti >= 0) & (ti < CHUNK)
    sel_cum = jnp.where(in_chunk & (ti <= tj), 1.0, 0.0).astype(BF16)
    sel_all = jnp.where(in_chunk, 1.0, 0.0).astype(BF16)
    zero_rhs = jnp.zeros((CHUNK, 2 * DN_HEAD_DIM), BF16)

    gc_tok, beta_tok, gc_head2, gl_head, egl_head = [], [], [], [], []
    for b in range(batch):
        ab = ab_ref[b]
        g_tok = -jnp.exp(prow_ref[0:1, :]) * _softplus(ab + prow_ref[1:2, :])
        beta_tok.append(jax.nn.sigmoid(ab))
        gc_tok.append(sum(_dot(tri, piece) for piece in _split3(g_tok)))
        g_head = -jnp.exp(pcol_ref[:, 0:1]) * _softplus(abt_ref[b, 0:DN_HEADS, :] + pcol_ref[:, 1:2])
        g_head3 = _split3(g_head)
        gc_head2.append(sum(_dot(piece, sel_cum) for piece in g_head3))
        gl = sum(_dot(piece, sel_all) for piece in g_head3)
        gl_head.append(gl)
        egl_head.append(jnp.exp(gl))

    items = [(b, h) for b in range(batch) for h in range(DN_HEADS)]

    def head_cols(h):
        return slice(h * DN_HEAD_DIM, (h + 1) * DN_HEAD_DIM)

    st = []
    for b, h in items:
        hs = head_cols(h)
        gcol = jnp.broadcast_to(gc_tok[b][:, h:h + 1], (CHUNK, LANES))
        beta = jnp.broadcast_to(beta_tok[b][:, DN_HEADS + h:DN_HEADS + h + 1], (CHUNK, LANES))
        q_b = q_ref[b, :, hs]
        k_b = k_ref[b, :, hs]
        k_f = k_b.astype(F32)
        kb = k_f * beta
        vb = v_ref[b, :, hs].astype(F32) * beta
        prod = _dot_nt(jnp.concatenate([kb.astype(BF16), q_b], axis=0),
                       jnp.concatenate([k_b, k_b], axis=0))
        st.append(dict(gcol=gcol, q_b=q_b, k_f=k_f, kb=kb, vb=vb, prod=prod))

    for (b, h), s in zip(items, st):
        decay2 = jnp.exp(jnp.where(incl2, s["gcol"] - gc_head2[b][h:h + 1, :], NEG_INF))
        l2 = jnp.where(strict2, s["prod"][:CHUNK] * decay2, 0.0)
        s["qk"] = (s["prod"][CHUNK:] * decay2)[:, :CHUNK].astype(BF16)
        l2b = l2.astype(BF16)
        s["x"] = jnp.where(hi_half, eye_hi - l2, _dot(l2b[:, :CHUNK], l2b))
        del s["prod"]

    for _ in range(4):
        for s in st:
            xb = s["x"].astype(BF16)
            s["x"] = _dot(xb[:, :CHUNK], xb) + jnp.where(hi_half, s["x"], 0.0)

    for s in st:
        xb = s["x"].astype(BF16)
        s["e"] = jnp.where(hi_half, s["x"] + _dot(xb[:, :CHUNK], xb) - eye_hi, 0.0).astype(BF16)
        del s["x"]

    for s in st:
        egc = jnp.exp(s["gcol"])
        rhs = jnp.concatenate([s["vb"], s["kb"] * egc], axis=-1)
        sol = rhs + _dot(s["e"], jnp.concatenate([zero_rhs, rhs.astype(BF16)], axis=0))
        s["u"] = sol[:, :DN_HEAD_DIM]
        q_dec = s["q_b"].astype(F32) * egc
        s["wq"] = jnp.concatenate([sol[:, DN_HEAD_DIM:].astype(BF16), q_dec.astype(BF16)], axis=0)
        del s["e"], s["vb"], s["kb"], s["q_b"]

    for i, s in enumerate(st):
        s["ws"] = _dot(s["wq"], state_ref[i].astype(BF16))
        del s["wq"]

    for i, ((b, h), s) in enumerate(zip(items, st)):
        v_new_b = (s["u"] - s["ws"][:CHUNK]).astype(BF16)
        o = s["ws"][CHUNK:] + _dot(s["qk"], v_new_b)
        gl_row = gl_head[b][h:h + 1, :]
        k_dec = s["k_f"] * jnp.exp(gl_row - s["gcol"])
        state_ref[i] = state_ref[i] * egl_head[b][h:h + 1, :] + _dot_tn(k_dec.astype(BF16), v_new_b)
        o = _rms(o, hn_ref[...])
        z = z_ref[b, :, head_cols(h)].astype(F32)
        o_ref[b, :, head_cols(h)] = (o * (z * jax.nn.sigmoid(z))).astype(BF16)


def _delta_rule(q, k, v, z, ab, abt, prow, pcol, hn):
    batch, seq, _ = q.shape
    row_spec = pl.BlockSpec((batch, CHUNK, D_MODEL), lambda c: (0, c, 0))
    return pl.pallas_call(
        _delta_kernel,
        grid=(seq // CHUNK,),
        in_specs=[row_spec, row_spec, row_spec, row_spec,
                  pl.BlockSpec((batch, CHUNK, LANES), lambda c: (0, c, 0)),
                  pl.BlockSpec((batch, 2 * DN_HEADS, DN_GATE_BLOCK), lambda c: (0, 0, c // 2)),
                  _resident((SUBLANES, LANES)),
                  _resident((DN_HEADS, 2)),
                  _resident((1, DN_HEAD_DIM))],
        out_specs=row_spec,
        out_shape=jax.ShapeDtypeStruct((batch, seq, D_MODEL), BF16),
        scratch_shapes=[pltpu.VMEM((batch * DN_HEADS, DN_HEAD_DIM, DN_HEAD_DIM), F32)],
        compiler_params=_params("arbitrary"),
        name="delta_rule",
    )(q, k, v, z, ab, abt, prow, pcol, hn)


def kernel(x, attn_norm, attn_w_qkv, attn_rel_bias, attn_w_o, dn_norm, dn_w_in, dn_conv_w,
           dn_a_log, dn_dt_bias, dn_head_norm, dn_w_o, mlp_norm, mlp_w_up, mlp_w_down, final_norm):
    batch, seq, d = x.shape
    assert d == D_MODEL and seq % ROW_TILE == 0 and seq % DN_GATE_BLOCK == 0 and seq % ATTN_Q == 0
    assert attn_norm.shape[0] == 1 and dn_norm.shape[0] == 1 and mlp_norm.shape[0] == 2
    t = batch * seq
    x2 = x.reshape(t, d)

    qkv = _qkv_proj(x2, attn_norm[0][None], attn_w_qkv[0].astype(BF16))
    att = _attention(qkv.reshape(batch, seq, 3 * d), _attn_rel_rows(attn_rel_bias[0]))
    x2 = _proj_mlp(x2, att.reshape(t, d), attn_w_o[0].astype(BF16), mlp_norm[0][None],
                   mlp_w_up[0].astype(BF16), mlp_w_down[0].astype(BF16))

    w_in = dn_w_in[0]
    w_ab = w_in[:, 4 * d:]
    w_ab_pad = jnp.pad(w_ab, ((0, 0), (0, LANES - 2 * DN_HEADS))).astype(BF16)
    q, k, v, z, ab, abt = _dn_proj(x2, dn_norm[0][None], w_in[:, :4 * d].astype(BF16), w_ab_pad,
                                   w_ab.T.astype(BF16), dn_conv_w[0], seq)
    prow = jnp.zeros((SUBLANES, LANES), F32)
    prow = prow.at[0, :DN_HEADS].set(dn_a_log[0]).at[1, :DN_HEADS].set(dn_dt_bias[0])
    pcol = jnp.stack([dn_a_log[0], dn_dt_bias[0]], axis=1)
    q, k, v, z = (a.reshape(batch, seq, d) for a in (q, k, v, z))
    og = _delta_rule(q, k, v, z, ab.reshape(batch, seq, LANES), abt, prow, pcol, dn_head_norm[0][None])
    x2 = _proj_mlp(x2, og.reshape(t, d), dn_w_o[0].astype(BF16), mlp_norm[1][None],
                   mlp_w_up[1].astype(BF16), mlp_w_down[1].astype(BF16), final_g=final_norm[None])
    return x2.reshape(batch, seq, d)
```

```python
import functools

import jax
import jax.numpy as jnp
from jax import lax
from jax.experimental import pallas as pl
from jax.experimental.pallas import tpu as pltpu

D_MODEL = 1024
CHUNK = 64
LEFT_CHUNKS = 8
ATTN_HEAD_DIM = 64
ATTN_HEADS = D_MODEL // ATTN_HEAD_DIM
MAX_REL = 256
DN_HEAD_DIM = 128
DN_HEADS = D_MODEL // DN_HEAD_DIM
CONV_K = 4
D_FF = 4 * D_MODEL
EPS = 1e-6

LANES = 128
SUBLANES = 8
VMEM_LIMIT = 56 * 1024 * 1024

ROW_TILE = 512
FF_TILE = 1024
ATTN_Q = 2 * CHUNK
ATTN_KV = ATTN_Q + LEFT_CHUNKS * CHUNK
ATTN_PAD = LEFT_CHUNKS * CHUNK
ATTN_REL = ATTN_KV + ATTN_Q
ATTN_GROUP = 4
DN_GATE_BLOCK = 2 * CHUNK

F32 = jnp.float32
BF16 = jnp.bfloat16
NEG_INF = float("-inf")
LOG2E = 1.4426950408889634


def _rms(x, g):
    ms = jnp.mean(x * x, axis=-1, keepdims=True)
    return (x * lax.rsqrt(ms + EPS)) * g


def _dot(a, b):
    return jnp.dot(a, b, preferred_element_type=F32)


def _dot_nt(a, b):
    return lax.dot_general(a, b, (((1,), (1,)), ((), ())), preferred_element_type=F32)


def _dot_tn(a, b):
    return lax.dot_general(a, b, (((0,), (0,)), ((), ())), preferred_element_type=F32)


def _resident(shape):
    return pl.BlockSpec(shape, lambda *_: (0,) * len(shape), pipeline_mode=pl.Buffered(1))


def _params(*sem):
    return pltpu.CompilerParams(dimension_semantics=sem, vmem_limit_bytes=VMEM_LIMIT)


def _qkv_kernel(x_ref, g_ref, w_ref, o_ref):
    xn = _rms(x_ref[...], g_ref[...]).astype(BF16)
    for j in range(3):
        cols = slice(j * D_MODEL, (j + 1) * D_MODEL)
        r = _dot(xn, w_ref[:, cols])
        if j == 0:
            r = r * (ATTN_HEAD_DIM ** -0.5 * LOG2E)
        o_ref[:, cols] = r.astype(BF16)


def _qkv_proj(x2, g, w):
    t = x2.shape[0]
    return pl.pallas_call(
        _qkv_kernel,
        grid=(t // ROW_TILE,),
        in_specs=[pl.BlockSpec((ROW_TILE, D_MODEL), lambda i: (i, 0)),
                  _resident((1, D_MODEL)),
                  _resident((D_MODEL, 3 * D_MODEL))],
        out_specs=pl.BlockSpec((ROW_TILE, 3 * D_MODEL), lambda i: (i, 0)),
        out_shape=jax.ShapeDtypeStruct((t, 3 * D_MODEL), BF16),
        compiler_params=_params("arbitrary"),
        name="qkv_proj",
    )(x2, g, w)


def _attn_kernel(q_ref, k_ref, v_ref, rel_ref, o_ref, kpad_ref, vpad_ref, bias_ref):
    seq = q_ref.shape[0]

    @pl.when(pl.program_id(1) == 0)
    def _():
        qq = lax.broadcasted_iota(jnp.int32, (ATTN_Q, ATTN_KV), 0)
        kb = lax.broadcasted_iota(jnp.int32, (ATTN_Q, ATTN_KV), 1)
        lag = kb // CHUNK - qq // CHUNK
        visible = (lag >= 0) & (lag <= LEFT_CHUNKS)
        for e in range(2):
            row = jnp.broadcast_to(rel_ref[e:e + 1, :], (ATTN_Q, ATTN_REL))
            skew = pltpu.roll(row, 0, 1, stride=1, stride_axis=0)
            bias_ref[e * ATTN_Q:(e + 1) * ATTN_Q, :] = jnp.where(visible, skew[:, :ATTN_KV] * LOG2E, NEG_INF)

    zeros = jnp.zeros((ATTN_PAD, LANES), BF16)
    kpad_ref[0:ATTN_PAD, :] = zeros
    vpad_ref[0:ATTN_PAD, :] = zeros
    kpad_ref[ATTN_PAD:, :] = k_ref[...]
    vpad_ref[ATTN_PAD:, :] = v_ref[...]
    first_head = lax.broadcasted_iota(jnp.int32, (ATTN_Q, LANES), 1) < ATTN_HEAD_DIM
    key_pos = lax.broadcasted_iota(jnp.int32, (2 * ATTN_Q, ATTN_KV), 1)

    def group(base, masked):
        rows = [base + g * ATTN_Q for g in range(ATTN_GROUP)]
        if not masked:
            rows = [pl.multiple_of(r, ATTN_Q) for r in rows]
        scores = []
        for r0 in rows:
            q2 = q_ref[pl.ds(r0, ATTN_Q), :]
            zq = jnp.zeros_like(q2)
            qs = jnp.concatenate([jnp.where(first_head, q2, zq), jnp.where(first_head, zq, q2)], axis=0)
            s = _dot_nt(qs, kpad_ref[pl.ds(r0, ATTN_KV), :]) + bias_ref[...]
            if masked:
                s = jnp.where(key_pos >= ATTN_PAD - r0, s, NEG_INF)
            scores.append(s)
        probs = []
        for s in scores:
            p = jnp.exp2(s - jnp.max(s, axis=-1, keepdims=True))
            probs.append((p.astype(BF16), jnp.sum(p, axis=-1, keepdims=True)))
        for r0, (p, l) in zip(rows, probs):
            o = _dot(p, vpad_ref[pl.ds(r0, ATTN_KV), :]) / l
            o_ref[pl.ds(r0, ATTN_Q), :] = jnp.where(first_head, o[:ATTN_Q], o[ATTN_Q:]).astype(BF16)

    span = ATTN_GROUP * ATTN_Q
    n_masked = -(-ATTN_PAD // span)
    for gi in range(n_masked):
        group(gi * span, True)

    def body(gi, c):
        group(gi * span, False)
        return c

    lax.fori_loop(n_masked, seq // span, body, 0)


def _attention(qkv, rel):
    b, s, _ = qkv.shape
    pairs = ATTN_HEADS // 2

    def col(off):
        return pl.BlockSpec((None, s, LANES), lambda hp, bi: (bi, 0, off + hp))

    return pl.pallas_call(
        _attn_kernel,
        grid=(pairs, b),
        in_specs=[col(0), col(pairs), col(2 * pairs),
                  pl.BlockSpec((None, 2, ATTN_REL), lambda hp, bi: (hp, 0, 0))],
        out_specs=pl.BlockSpec((None, s, LANES), lambda hp, bi: (bi, 0, hp)),
        out_shape=jax.ShapeDtypeStruct((b, s, D_MODEL), BF16),
        scratch_shapes=[pltpu.VMEM((s + ATTN_PAD, LANES), BF16),
                        pltpu.VMEM((s + ATTN_PAD, LANES), BF16),
                        pltpu.VMEM((2 * ATTN_Q, ATTN_KV), F32)],
        compiler_params=_params("arbitrary", "arbitrary"),
        name="band_attention",
    )(qkv, qkv, qkv, rel)


def _attn_rel_rows(rel_bias):
    far = rel_bias[:, 2 * MAX_REL:]
    near = rel_bias[:, 2 * MAX_REL - 1:MAX_REL - ATTN_Q:-1]
    n_far = ATTN_PAD - MAX_REL + 1
    rows = jnp.concatenate(
        [jnp.broadcast_to(far, (ATTN_HEADS, n_far)), near,
         jnp.broadcast_to(far, (ATTN_HEADS, ATTN_REL - n_far - near.shape[1]))], axis=1)
    return rows.astype(F32).reshape(ATTN_HEADS // 2, 2, ATTN_REL)


def _proj_mlp_kernel(*refs, final):
    if final:
        x_ref, a_ref, wo_ref, g_ref, wup_ref, wdn_ref, fg_ref, o_ref, act_ref = refs
    else:
        x_ref, a_ref, wo_ref, g_ref, wup_ref, wdn_ref, o_ref, act_ref = refs
    o_ref[...] = x_ref[...] + _dot(a_ref[...], wo_ref[...])
    hn = _rms(o_ref[...], g_ref[...]).astype(BF16)
    for f in range(D_FF // FF_TILE):
        cols = slice(f * FF_TILE, (f + 1) * FF_TILE)
        u = jnp.maximum(_dot(hn, wup_ref[:, cols]), 0.0)
        act_ref[:, cols] = (u * u).astype(BF16)
    y = o_ref[...] + _dot(act_ref[...], wdn_ref[...])
    if final:
        y = _rms(y, fg_ref[...])
    o_ref[...] = y


def _proj_mlp(x2, a2, wo, g, wup, wdn, final_g=None):
    t = x2.shape[0]
    final = final_g is not None
    in_specs = [pl.BlockSpec((ROW_TILE, D_MODEL), lambda i: (i, 0)),
                pl.BlockSpec((ROW_TILE, D_MODEL), lambda i: (i, 0)),
                _resident((D_MODEL, D_MODEL)),
                _resident((1, D_MODEL)),
                _resident((D_MODEL, D_FF)),
                _resident((D_FF, D_MODEL))]
    args = [x2, a2, wo, g, wup, wdn]
    if final:
        in_specs.append(_resident((1, D_MODEL)))
        args.append(final_g)
    return pl.pallas_call(
        functools.partial(_proj_mlp_kernel, final=final),
        grid=(t // ROW_TILE,),
        in_specs=in_specs,
        out_specs=pl.BlockSpec((ROW_TILE, D_MODEL), lambda i: (i, 0)),
        out_shape=jax.ShapeDtypeStruct((t, D_MODEL), F32),
        scratch_shapes=[pltpu.VMEM((ROW_TILE, D_FF), BF16)],
        compiler_params=_params("arbitrary"),
        name="proj_mlp_final" if final else "proj_mlp",
    )(*args)


def _dn_proj_kernel(x_ref, g_ref, w_ref, wab_ref, wabt_ref, cw_ref,
                    q_ref, k_ref, v_ref, z_ref, ab_ref, abt_ref, buf_ref, *, tiles_per_seq):
    i = pl.program_id(0)
    xn = _rms(x_ref[...], g_ref[...]).astype(BF16)
    rows = x_ref.shape[0]
    halo = CONV_K - 1

    @pl.when(i % tiles_per_seq == 0)
    def _():
        buf_ref[:, 0:SUBLANES, :] = jnp.zeros((3, SUBLANES, D_MODEL), F32)

    outs = (q_ref, k_ref, v_ref)
    for j in range(3):
        cols = slice(j * D_MODEL, (j + 1) * D_MODEL)
        buf_ref[j, SUBLANES:SUBLANES + rows, :] = _dot(xn, w_ref[:, cols])
        y = None
        for tap in range(CONV_K):
            start = SUBLANES - halo + tap
            term = buf_ref[j, start:start + rows, :] * cw_ref[tap:tap + 1, cols]
            y = term if y is None else y + term
        buf_ref[j, SUBLANES - halo:SUBLANES, :] = buf_ref[j, SUBLANES + rows - halo:SUBLANES + rows, :]
        y = y * jax.nn.sigmoid(y)
        if j < 2:
            for h in range(DN_HEADS):
                hs = slice(h * DN_HEAD_DIM, (h + 1) * DN_HEAD_DIM)
                yh = y[:, hs]
                yh = yh * lax.rsqrt(jnp.sum(yh * yh, axis=-1, keepdims=True) + EPS)
                if j == 0:
                    yh = yh * (DN_HEAD_DIM ** -0.5)
                outs[j][:, hs] = yh.astype(BF16)
        else:
            outs[j][...] = y.astype(BF16)
    z_ref[...] = _dot(xn, w_ref[:, 3 * D_MODEL:4 * D_MODEL]).astype(BF16)
    ab_ref[...] = _dot(xn, wab_ref[...])
    abt_ref[...] = _dot_nt(wabt_ref[...], xn)


def _dn_proj(x2, g, w_main, w_ab, w_abt, conv_w, seq):
    t = x2.shape[0]
    tiles_per_seq = seq // ROW_TILE
    row_spec = pl.BlockSpec((ROW_TILE, D_MODEL), lambda i: (i, 0))
    act = jax.ShapeDtypeStruct((t, D_MODEL), BF16)
    return pl.pallas_call(
        functools.partial(_dn_proj_kernel, tiles_per_seq=tiles_per_seq),
        grid=(t // ROW_TILE,),
        in_specs=[row_spec,
                  _resident((1, D_MODEL)),
                  _resident((D_MODEL, 4 * D_MODEL)),
                  _resident((D_MODEL, LANES)),
                  _resident((2 * DN_HEADS, D_MODEL)),
                  _resident((CONV_K, 3 * D_MODEL))],
        out_specs=[row_spec, row_spec, row_spec, row_spec,
                   pl.BlockSpec((ROW_TILE, LANES), lambda i: (i, 0)),
                   pl.BlockSpec((None, 2 * DN_HEADS, ROW_TILE),
                                lambda i: (i // tiles_per_seq, 0, i % tiles_per_seq))],
        out_shape=[act, act, act, act,
                   jax.ShapeDtypeStruct((t, LANES), F32),
                   jax.ShapeDtypeStruct((t // seq, 2 * DN_HEADS, seq), F32)],
        scratch_shapes=[pltpu.VMEM((3, SUBLANES + ROW_TILE, D_MODEL), F32)],
        compiler_params=_params("arbitrary"),
        name="dn_proj",
    )(x2, g, w_main, w_ab, w_abt, conv_w)


def _split3(a):
    a1 = a.astype(BF16)
    r = a - a1.astype(F32)
    a2 = r.astype(BF16)
    a3 = (r - a2.astype(F32)).astype(BF16)
    return a1, a2, a3


def _softplus(x):
    return jnp.maximum(x, 0.0) + jnp.log1p(jnp.exp(-jnp.abs(x)))


def _delta_kernel(q_ref, k_ref, v_ref, z_ref, ab_ref, abt_ref, prow_ref, pcol_ref, hn_ref,
                  o_ref, state_ref):
    batch = q_ref.shape[0]
    c = pl.program_id(0)

    @pl.when(c == 0)
    def _():
        state_ref[...] = jnp.zeros_like(state_ref)

    ri = lax.broadcasted_iota(jnp.int32, (CHUNK, LANES), 0)
    li = lax.broadcasted_iota(jnp.int32, (CHUNK, LANES), 1)
    lj = li % CHUNK
    hi_half = li >= CHUNK
    incl2 = ri >= lj
    strict2 = ri > lj
    eye_hi = jnp.where(hi_half & (ri == lj), 1.0, 0.0)
    ci = lax.broadcasted_iota(jnp.int32, (CHUNK, CHUNK), 0)
    cj = lax.broadcasted_iota(jnp.int32, (CHUNK, CHUNK), 1)
    tri = jnp.where(ci >= cj, 1.0, 0.0).astype(BF16)
    ti = lax.broadcasted_iota(jnp.int32, (DN_GATE_BLOCK, LANES), 0) - (c % 2) * CHUNK
    tj = lax.broadcasted_iota(jnp.int32, (DN_GATE_BLOCK, LANES), 1) % CHUNK
    in_chunk = (ti >= 0) & (ti < CHUNK)
    sel_cum = jnp.where(in_chunk & (ti <= tj), 1.0, 0.0).astype(BF16)
    sel_all = jnp.where(in_chunk, 1.0, 0.0).astype(BF16)
    zero_rhs = jnp.zeros((CHUNK, 2 * DN_HEAD_DIM), BF16)

    gc_tok, beta_tok, gc_head2, gl_head, egl_head = [], [], [], [], []
    for b in range(batch):
        ab = ab_ref[b]
        g_tok = -jnp.exp(prow_ref[0:1, :]) * _softplus(ab + prow_ref[1:2, :])
        beta_tok.append(jax.nn.sigmoid(ab))
        gc_tok.append(sum(_dot(tri, piece) for piece in _split3(g_tok)))
        g_head = -jnp.exp(pcol_ref[:, 0:1]) * _softplus(abt_ref[b, 0:DN_HEADS, :] + pcol_ref[:, 1:2])
        g_head3 = _split3(g_head)
        gc_head2.append(sum(_dot(piece, sel_cum) for piece in g_head3))
        gl = sum(_dot(piece, sel_all) for piece in g_head3)
        gl_head.append(gl)
        egl_head.append(jnp.exp(gl))

    items = [(b, h) for b in range(batch) for h in range(DN_HEADS)]

    def head_cols(h):
        return slice(h * DN_HEAD_DIM, (h + 1) * DN_HEAD_DIM)

    st = []
    for b, h in items:
        hs = head_cols(h)
        gcol = jnp.broadcast_to(gc_tok[b][:, h:h + 1], (CHUNK, LANES))
        beta = jnp.broadcast_to(beta_tok[b][:, DN_HEADS + h:DN_HEADS + h + 1], (CHUNK, LANES))
        q_b = q_ref[b, :, hs]
        k_b = k_ref[b, :, hs]
        k_f = k_b.astype(F32)
        kb = k_f * beta
        vb = v_ref[b, :, hs].astype(F32) * beta
        prod = _dot_nt(jnp.concatenate([kb.astype(BF16), q_b], axis=0),
                       jnp.concatenate([k_b, k_b], axis=0))
        st.append(dict(gcol=gcol, q_b=q_b, k_f=k_f, kb=kb, vb=vb, prod=prod))

    for (b, h), s in zip(items, st):
        decay2 = jnp.exp(jnp.where(incl2, s["gcol"] - gc_head2[b][h:h + 1, :], NEG_INF))
        l2 = jnp.where(strict2, s["prod"][:CHUNK] * decay2, 0.0)
        s["qk"] = (s["prod"][CHUNK:] * decay2)[:, :CHUNK].astype(BF16)
        l2b = l2.astype(BF16)
        s["x"] = jnp.where(hi_half, eye_hi - l2, _dot(l2b[:, :CHUNK], l2b))
        del s["prod"]

    for _ in range(4):
        for s in st:
            xb = s["x"].astype(BF16)
            s["x"] = _dot(xb[:, :CHUNK], xb) + jnp.where(hi_half, s["x"], 0.0)

    for s in st:
        xb = s["x"].astype(BF16)
        s["e"] = jnp.where(hi_half, s["x"] + _dot(xb[:, :CHUNK], xb) - eye_hi, 0.0).astype(BF16)
        del s["x"]

    for s in st:
        egc = jnp.exp(s["gcol"])
        rhs = jnp.concatenate([s["vb"], s["kb"] * egc], axis=-1)
        sol = rhs + _dot(s["e"], jnp.concatenate([zero_rhs, rhs.astype(BF16)], axis=0))
        s["u"] = sol[:, :DN_HEAD_DIM]
        q_dec = s["q_b"].astype(F32) * egc
        s["wq"] = jnp.concatenate([sol[:, DN_HEAD_DIM:].astype(BF16), q_dec.astype(BF16)], axis=0)
        del s["e"], s["vb"], s["kb"], s["q_b"]

    for i, s in enumerate(st):
        s["ws"] = _dot(s["wq"], state_ref[i].astype(BF16))
        del s["wq"]

    for i, ((b, h), s) in enumerate(zip(items, st)):
        v_new_b = (s["u"] - s["ws"][:CHUNK]).astype(BF16)
        o = s["ws"][CHUNK:] + _dot(s["qk"], v_new_b)
        gl_row = gl_head[b][h:h + 1, :]
        k_dec = s["k_f"] * jnp.exp(gl_row - s["gcol"])
        state_ref[i] = state_ref[i] * egl_head[b][h:h + 1, :] + _dot_tn(k_dec.astype(BF16), v_new_b)
        o = _rms(o, hn_ref[...])
        z = z_ref[b, :, head_cols(h)].astype(F32)
        o_ref[b, :, head_cols(h)] = (o * (z * jax.nn.sigmoid(z))).astype(BF16)


def _delta_rule(q, k, v, z, ab, abt, prow, pcol, hn):
    batch, seq, _ = q.shape
    row_spec = pl.BlockSpec((batch, CHUNK, D_MODEL), lambda c: (0, c, 0))
    return pl.pallas_call(
        _delta_kernel,
        grid=(seq // CHUNK,),
        in_specs=[row_spec, row_spec, row_spec, row_spec,
                  pl.BlockSpec((batch, CHUNK, LANES), lambda c: (0, c, 0)),
                  pl.BlockSpec((batch, 2 * DN_HEADS, DN_GATE_BLOCK), lambda c: (0, 0, c // 2)),
                  _resident((SUBLANES, LANES)),
                  _resident((DN_HEADS, 2)),
                  _resident((1, DN_HEAD_DIM))],
        out_specs=row_spec,
        out_shape=jax.ShapeDtypeStruct((batch, seq, D_MODEL), BF16),
        scratch_shapes=[pltpu.VMEM((batch * DN_HEADS, DN_HEAD_DIM, DN_HEAD_DIM), F32)],
        compiler_params=_params("arbitrary"),
        name="delta_rule",
    )(q, k, v, z, ab, abt, prow, pcol, hn)


def kernel(x, attn_norm, attn_w_qkv, attn_rel_bias, attn_w_o, dn_norm, dn_w_in, dn_conv_w,
           dn_a_log, dn_dt_bias, dn_head_norm, dn_w_o, mlp_norm, mlp_w_up, mlp_w_down, final_norm):
    batch, seq, d = x.shape
    assert d == D_MODEL and seq % ROW_TILE == 0 and seq % DN_GATE_BLOCK == 0 and seq % ATTN_Q == 0
    assert attn_norm.shape[0] == 1 and dn_norm.shape[0] == 1 and mlp_norm.shape[0] == 2
    t = batch * seq
    x2 = x.reshape(t, d)

    qkv = _qkv_proj(x2, attn_norm[0][None], attn_w_qkv[0].astype(BF16))
    att = _attention(qkv.reshape(batch, seq, 3 * d), _attn_rel_rows(attn_rel_bias[0]))
    x2 = _proj_mlp(x2, att.reshape(t, d), attn_w_o[0].astype(BF16), mlp_norm[0][None],
                   mlp_w_up[0].astype(BF16), mlp_w_down[0].astype(BF16))

    w_in = dn_w_in[0]
    w_ab = w_in[:, 4 * d:]
    w_ab_pad = jnp.pad(w_ab, ((0, 0), (0, LANES - 2 * DN_HEADS))).astype(BF16)
    q, k, v, z, ab, abt = _dn_proj(x2, dn_norm[0][None], w_in[:, :4 * d].astype(BF16), w_ab_pad,
                                   w_ab.T.astype(BF16), dn_conv_w[0], seq)
    prow = jnp.zeros((SUBLANES, LANES), F32)
    prow = prow.at[0, :DN_HEADS].set(dn_a_log[0]).at[1, :DN_HEADS].set(dn_dt_bias[0])
    pcol = jnp.stack([dn_a_log[0], dn_dt_bias[0]], axis=1)
    q, k, v, z = (a.reshape(batch, seq, d) for a in (q, k, v, z))
    og = _delta_rule(q, k, v, z, ab.reshape(batch, seq, LANES), abt, prow, pcol, dn_head_norm[0][None])
    x2 = _proj_mlp(x2, og.reshape(t, d), dn_w_o[0].astype(BF16), mlp_norm[1][None],
                   mlp_w_up[1].astype(BF16), mlp_w_down[1].astype(BF16), final_g=final_norm[None])
    return x2.reshape(batch, seq, d)
```
